```python
import math
import jax, jax.numpy as jnp
from jax import lax
import numpy as np

D_MODEL = 1024
BATCH = 16
SEQ = 2048
DEPTH = 1
DEC_BATCH = 32
DEC_SEQ = 32
PAST_LEN = 4096

CHUNK = 64
Q_BLOCK = 128
HEAD_DIM = 64
SSM_HEADS = 8
SSM_WIDTH = SSM_HEADS * HEAD_DIM
SSM_GROUPS = 2
D_STATE = 128
D_CONV = 4
CONV_DIM = SSM_WIDTH + 2 * SSM_GROUPS * D_STATE
SB_HEADS = 8
SB_WIDTH = SB_HEADS * HEAD_DIM
MIX_WIDTH = SSM_WIDTH + SB_WIDTH
SB_SCALE = HEAD_DIM ** -0.5
OFF_XBC = SSM_WIDTH
OFF_DT = OFF_XBC + CONV_DIM
OFF_Q = OFF_DT + SSM_HEADS
OFF_K = OFF_Q + SB_WIDTH
OFF_V = OFF_K + SB_WIDTH
IN_COLS = OFF_V + SB_WIDTH
PEER_HEADS = 8
N_KEYS = 128
N_EXPERTS = N_KEYS * N_KEYS
D_KEY = 256
HALF_KEY = D_KEY // 2
PEER_TOPK = 16
PEER_ROW = 128
EPS = 1e-6

kernel_name = 'hybrid_ssd_stickbreaking_peer'


def rmsnorm(x, w):
    xf = x.astype(jnp.float32)
    y = xf * lax.rsqrt(jnp.mean(xf * xf, axis=-1, keepdims=True) + EPS)
    return (y * w.astype(jnp.float32)).astype(x.dtype)


def causal_conv(xbc, conv_state, conv_w, conv_b):
    L = xbc.shape[1]
    full = jnp.concatenate([conv_state.astype(xbc.dtype), xbc], axis=1)
    out = conv_b
    for j in range(D_CONV):
        out = out + full[:, j:j + L, :] * conv_w[j]
    new_state = full[:, full.shape[1] - (D_CONV - 1):, :]
    return jax.nn.silu(out), new_state


def ssd_scan(x, dt, a, bm, cm, init_state, chunk):
    f32 = jnp.float32
    b, L, h, p = x.shape
    nc = L // chunk
    rep = h // bm.shape[2]
    bh = jnp.repeat(bm.astype(f32), rep, axis=2).reshape(b, nc, chunk, h, D_STATE)
    ch = jnp.repeat(cm.astype(f32), rep, axis=2).reshape(b, nc, chunk, h, D_STATE)
    xdt = (x.astype(f32) * dt[..., None]).reshape(b, nc, chunk, h, p)
    a_cum = jnp.cumsum((dt * a).reshape(b, nc, chunk, h), axis=2)
    causal = jnp.tril(jnp.ones((chunk, chunk), dtype=bool))[None, None, :, :, None]
    seg = a_cum[:, :, :, None, :] - a_cum[:, :, None, :, :]
    decay = jnp.where(causal, jnp.exp(jnp.where(causal, seg, 0.0)), 0.0)
    scores = jnp.einsum('bclhn,bcshn->bclsh', ch, bh) * decay
    y_diag = jnp.einsum('bclsh,bcshp->bclhp', scores, xdt)
    to_end = jnp.exp(a_cum[:, :, -1:, :] - a_cum)
    chunk_states = jnp.einsum('bclhn,bclh,bclhp->bchpn', bh, to_end, xdt)
    chunk_decay = jnp.exp(a_cum[:, :, -1, :])

    def carry_step(state, inp):
        st, dec = inp
        return dec[:, :, None, None] * state + st, state

    final, prev = lax.scan(carry_step, init_state.astype(f32),
                           (jnp.moveaxis(chunk_states, 1, 0), jnp.moveaxis(chunk_decay, 1, 0)))
    prev = jnp.moveaxis(prev, 0, 1)
    y_off = jnp.einsum('bclhn,bchpn,bclh->bclhp', ch, prev, jnp.exp(a_cum))
    return (y_diag + y_off).reshape(b, L, h, p), final


def ssm_group(z, xbc, dt_raw, conv_state, ssm_state, chunk, conv_w, conv_b, dt_bias, a_log, d_skip, ssm_norm_w):
    f32 = jnp.float32
    b, L, _ = z.shape
    xbc_a, new_conv = causal_conv(xbc, conv_state, conv_w, conv_b)
    xs = xbc_a[..., :SSM_WIDTH].reshape(b, L, SSM_HEADS, HEAD_DIM)
    bm = xbc_a[..., SSM_WIDTH:SSM_WIDTH + SSM_GROUPS * D_STATE].reshape(b, L, SSM_GROUPS, D_STATE)
    cm = xbc_a[..., SSM_WIDTH + SSM_GROUPS * D_STATE:].reshape(b, L, SSM_GROUPS, D_STATE)
    dt = jax.nn.softplus(dt_raw.astype(f32) + dt_bias.astype(f32))
    a = -jnp.exp(a_log.astype(f32))
    y, new_ssm = ssd_scan(xs, dt, a, bm, cm, ssm_state, chunk)
    y = y + d_skip.astype(f32)[:, None] * xs.astype(f32)
    y = y.reshape(b, L, SSM_WIDTH) * jax.nn.silu(z.astype(f32))
    yg = y.reshape(b, L, SSM_GROUPS, SSM_WIDTH // SSM_GROUPS)
    yg = yg * lax.rsqrt(jnp.mean(yg * yg, axis=-1, keepdims=True) + EPS)
    y = yg.reshape(b, L, SSM_WIDTH) * ssm_norm_w.astype(f32)
    return y.astype(z.dtype), new_conv, new_ssm.astype(ssm_state.dtype)


def stick_breaking(q, k, v, q_pos, k_pos):
    f32 = jnp.float32
    z = jnp.einsum('bqhd,bkhd->bhqk', q.astype(f32), k.astype(f32)) * SB_SCALE
    mask = k_pos[None, :] < q_pos[:, None]
    log_1m = jnp.where(mask, jax.nn.log_sigmoid(-z), 0.0)
    after = lax.cumsum(log_1m, axis=3, reverse=True) - log_1m
    w = jnp.where(mask, jnp.exp(jax.nn.log_sigmoid(z) + after), 0.0)
    return jnp.einsum('bhqk,bkhd->bqhd', w, v.astype(f32)).astype(v.dtype)


def sb_prompt(q, k, v):
    L = q.shape[1]
    pos = jnp.arange(L)
    outs = []
    for i in range(L // Q_BLOCK):
        lo, hi = i * Q_BLOCK, (i + 1) * Q_BLOCK
        outs.append(stick_breaking(q[:, lo:hi], k[:, :hi], v[:, :hi], pos[lo:hi], pos[:hi]))
    return jnp.concatenate(outs, axis=1)


def sb_cached(q, k, v, k_past, v_past):
    past = k_past.shape[1]
    L = q.shape[1]
    k_all = jnp.concatenate([k_past.astype(k.dtype), k], axis=1)
    v_all = jnp.concatenate([v_past.astype(v.dtype), v], axis=1)
    k_pos = jnp.arange(past + L)
    q_pos = past + jnp.arange(L)
    return stick_breaking(q, k_all, v_all, q_pos, k_pos)


def peer_rows(xr, wq, k1, k2, u_tab, v_tab):
    f32 = jnp.float32
    L = xr.shape[0]
    q = (xr @ wq).astype(f32).reshape(L, PEER_HEADS, D_KEY)
    s1 = jnp.einsum('lhk,hnk->lhn', q[..., :HALF_KEY], k1.astype(f32))
    s2 = jnp.einsum('lhk,hnk->lhn', q[..., HALF_KEY:], k2.astype(f32))
    v1, i1 = lax.top_k(s1, PEER_TOPK)
    v2, i2 = lax.top_k(s2, PEER_TOPK)
    cand = (v1[..., :, None] + v2[..., None, :]).reshape(L, PEER_HEADS, PEER_TOPK * PEER_TOPK)
    best, flat = lax.top_k(cand, PEER_TOPK)
    row = jnp.take_along_axis(i1, flat // PEER_TOPK, axis=-1)
    col = jnp.take_along_axis(i2, flat % PEER_TOPK, axis=-1)
    expert = row * N_KEYS + col
    gate = jax.nn.softmax(best, axis=-1)
    u_sel = jnp.take(u_tab, expert, axis=0)
    act = jax.nn.gelu(jnp.einsum('lhkd,ld->lhk', u_sel, xr).astype(f32), approximate=False)
    coef = (gate * act).astype(xr.dtype)
    return jnp.einsum('lhk,lhkd->ld', coef, jnp.take(v_tab, expert, axis=0))


def peer(x, row_len, wq, k1, k2, u_tab, v_tab):
    rows = x.reshape(-1, row_len, D_MODEL)
    out = lax.map(lambda xr: peer_rows(xr, wq, k1, k2, u_tab, v_tab), rows)
    return out.reshape(x.shape)


def layer(x, conv_state, ssm_state, k_past, v_past, chunk, row_len,
          norm1_w, w_in, conv_w, conv_b, dt_bias, a_log, d_skip, ssm_norm_w, w_out,
          norm2_w, peer_wq, peer_k1, peer_k2, peer_u, peer_v):
    b, L, _ = x.shape
    xn = rmsnorm(x, norm1_w)
    proj = xn @ w_in
    z = proj[..., :OFF_XBC]
    xbc = proj[..., OFF_XBC:OFF_DT]
    dt_raw = proj[..., OFF_DT:OFF_Q]
    q = proj[..., OFF_Q:OFF_K].reshape(b, L, SB_HEADS, HEAD_DIM)
    k = proj[..., OFF_K:OFF_V].reshape(b, L, SB_HEADS, HEAD_DIM)
    v = proj[..., OFF_V:].reshape(b, L, SB_HEADS, HEAD_DIM)
    y_ssm, new_conv, new_ssm = ssm_group(z, xbc, dt_raw, conv_state, ssm_state, chunk,
                                         conv_w, conv_b, dt_bias, a_log, d_skip, ssm_norm_w)
    if k_past is None:
        o_sb = sb_prompt(q, k, v)
    else:
        o_sb = sb_cached(q, k, v, k_past, v_past)
    mixed = jnp.concatenate([y_ssm, o_sb.reshape(b, L, SB_WIDTH).astype(y_ssm.dtype)], axis=-1)
    h = x + mixed @ w_out
    h = h + peer(rmsnorm(h, norm2_w), row_len, peer_wq, peer_k1, peer_k2, peer_u, peer_v)
    return h, k, v, new_ssm, new_conv


def setup_inputs(seed: int = 0) -> dict:
    key = jax.random.key(seed)
    ks = jax.random.split(key, 24)
    f32 = jnp.float32
    nrm = lambda k, shape, s: jax.random.normal(k, shape, f32) * s
    dt = jnp.exp(jax.random.uniform(ks[10], (DEPTH, SSM_HEADS), f32) * (math.log(0.1) - math.log(1e-3)) + math.log(1e-3))
    return {
        'x_prompt': nrm(ks[0], (BATCH, SEQ, D_MODEL), 1.0),
        'x_sample': nrm(ks[1], (DEC_BATCH, DEC_SEQ, D_MODEL), 1.0),
        'cache_sb_k': nrm(ks[2], (DEPTH, DEC_BATCH, PAST_LEN, SB_HEADS, HEAD_DIM), 1.0),
        'cache_sb_v': nrm(ks[3], (DEPTH, DEC_BATCH, PAST_LEN, SB_HEADS, HEAD_DIM), 1.0),
        'state_ssm': nrm(ks[4], (DEPTH, DEC_BATCH, SSM_HEADS, HEAD_DIM, D_STATE), 0.1),
        'state_conv': nrm(ks[5], (DEPTH, DEC_BATCH, D_CONV - 1, CONV_DIM), 1.0),
        'norm1_w': 1.0 + nrm(ks[6], (DEPTH, D_MODEL), 0.01),
        'w_in': nrm(ks[7], (DEPTH, D_MODEL, IN_COLS), D_MODEL ** -0.5),
        'conv_w': nrm(ks[8], (DEPTH, D_CONV, CONV_DIM), D_CONV ** -0.5),
        'conv_b': nrm(ks[9], (DEPTH, CONV_DIM), 0.01),
        'dt_bias': dt + jnp.log(-jnp.expm1(-dt)),
        'a_log': jnp.log(jax.random.uniform(ks[11], (DEPTH, SSM_HEADS), f32, 1.0, 16.0)),
        'd_skip': 1.0 + nrm(ks[12], (DEPTH, SSM_HEADS), 0.1),
        'ssm_norm_w': 1.0 + nrm(ks[13], (DEPTH, SSM_WIDTH), 0.01),
        'w_out': nrm(ks[14], (DEPTH, MIX_WIDTH, D_MODEL), MIX_WIDTH ** -0.5),
        'norm2_w': 1.0 + nrm(ks[15], (DEPTH, D_MODEL), 0.01),
        'peer_wq': nrm(ks[16], (DEPTH, D_MODEL, PEER_HEADS * D_KEY), D_MODEL ** -0.5),
        'peer_k1': nrm(ks[17], (DEPTH, PEER_HEADS, N_KEYS, HALF_KEY), HALF_KEY ** -0.5),
        'peer_k2': nrm(ks[18], (DEPTH, PEER_HEADS, N_KEYS, HALF_KEY), HALF_KEY ** -0.5),
        'peer_u': nrm(ks[19], (DEPTH, N_EXPERTS, D_MODEL), D_MODEL ** -0.5),
        'peer_v': nrm(ks[20], (DEPTH, N_EXPERTS, D_MODEL), PEER_HEADS ** -0.5),
        'final_norm_w': 1.0 + nrm(ks[21], (D_MODEL,), 0.01),
    }


def reference(x_prompt, x_sample, cache_sb_k, cache_sb_v, state_ssm, state_conv,
              norm1_w, w_in, conv_w, conv_b, dt_bias, a_log, d_skip, ssm_norm_w, w_out,
              norm2_w, peer_wq, peer_k1, peer_k2, peer_u, peer_v, final_norm_w):
    hp, hs = x_prompt, x_sample
    bp, bs, ls = x_prompt.shape[0], x_sample.shape[0], x_sample.shape[1]
    kp_l, vp_l, sp_l, cp_l, ks_l, vs_l, ss_l, cs_l = [], [], [], [], [], [], [], []
    for l in range(DEPTH):
        w = (norm1_w[l], w_in[l], conv_w[l], conv_b[l], dt_bias[l], a_log[l], d_skip[l],
             ssm_norm_w[l], w_out[l], norm2_w[l], peer_wq[l], peer_k1[l], peer_k2[l], peer_u[l], peer_v[l])
        conv0 = jnp.zeros((bp, D_CONV - 1, CONV_DIM), hp.dtype)
        ssm0 = jnp.zeros((bp, SSM_HEADS, HEAD_DIM, D_STATE), jnp.float32)
        hp, kp, vp, sp, cp = layer(hp, conv0, ssm0, None, None, CHUNK, PEER_ROW, *w)
        hs, ks_, vs_, ss, cs = layer(hs, state_conv[l], state_ssm[l], cache_sb_k[l], cache_sb_v[l], ls, ls, *w)
        kp_l.append(kp); vp_l.append(vp); sp_l.append(sp); cp_l.append(cp)
        ks_l.append(ks_); vs_l.append(vs_); ss_l.append(ss); cs_l.append(cs)
    y_prompt = rmsnorm(hp, final_norm_w)
    y_sample = rmsnorm(hs, final_norm_w)
    return (y_prompt, y_sample,
            jnp.stack(kp_l), jnp.stack(vp_l), jnp.stack(sp_l), jnp.stack(cp_l),
            jnp.stack(ks_l), jnp.stack(vs_l), jnp.stack(ss_l), jnp.stack(cs_l))
```

```python
import functools

import jax
import jax.numpy as jnp
from jax import lax
from jax.experimental import pallas as pl
from jax.experimental.pallas import tpu as pltpu

F32 = jnp.float32
BF16 = jnp.bfloat16
HIGHEST = lax.Precision.HIGHEST

D_MODEL = 1024
HEAD_DIM = 64
SSM_HEADS = 8
SSM_WIDTH = SSM_HEADS * HEAD_DIM
SSM_GROUPS = 2
GROUP_WIDTH = SSM_WIDTH // SSM_GROUPS
HEADS_PER_GROUP = SSM_HEADS // SSM_GROUPS
D_STATE = 128
D_CONV = 4
CONV_DIM = SSM_WIDTH + 2 * SSM_GROUPS * D_STATE
SB_HEADS = 8
SB_WIDTH = SB_HEADS * HEAD_DIM
SB_SCALE = HEAD_DIM ** -0.5
OFF_XBC = SSM_WIDTH
OFF_DT = OFF_XBC + CONV_DIM
OFF_Q = OFF_DT + SSM_HEADS
PEER_HEADS = 8
N_KEYS = 128
D_KEY = 256
HALF_KEY = D_KEY // 2
PEER_TOPK = 16
EPS = 1e-6

LANES = 128
SUBLANES = 8
VMEM_LIMIT = 56 * 1024 * 1024


def _softplus(x):
    return jnp.maximum(x, 0.0) + jnp.log(1.0 + jnp.exp(-jnp.abs(x)))


def _silu(x):
    return x * (1.0 / (1.0 + jnp.exp(-x)))


def _dot(a, b, **kw):
    return jnp.dot(a, b, preferred_element_type=F32, **kw)


def _dot_nt(a, b):
    return lax.dot_general(a, b, (((1,), (1,)), ((), ())), preferred_element_type=F32)


def _dot_tn(a, b):
    return lax.dot_general(a, b, (((0,), (0,)), ((), ())), preferred_element_type=F32)


def _split_bf16(x):
    hi = x.astype(BF16)
    lo = (x - hi.astype(F32)).astype(BF16)
    return hi, lo


def _inproj_body(x_ref, n1_ref, w_ref, wdt_ref, z_ref, xbc_ref, dt_ref, q_ref, k_ref, v_ref):
    x = x_ref[...]
    xn = x * lax.rsqrt(jnp.mean(x * x, axis=-1, keepdims=True) + EPS) * n1_ref[...]
    xb = xn.astype(BF16)
    o = 0
    for ref in (z_ref, xbc_ref, q_ref, k_ref, v_ref):
        wd = ref.shape[-1]
        ref[...] = _dot(xb, w_ref[:, o:o + wd])
        o += wd
    dt_ref[...] = _dot(xb, wdt_ref[...])


def _inproj(x2d, n1, w_main, w_dt, tm):
    n = x2d.shape[0]
    widths = (SSM_WIDTH, CONV_DIM, SB_WIDTH, SB_WIDTH, SB_WIDTH)
    out_w = (SSM_WIDTH, CONV_DIM, LANES, SB_WIDTH, SB_WIDTH, SB_WIDTH)
    row = lambda i: (i, 0)
    const = lambda i: (0, 0)
    return pl.pallas_call(
        _inproj_body,
        grid=(n // tm,),
        in_specs=[
            pl.BlockSpec((tm, D_MODEL), row),
            pl.BlockSpec((1, D_MODEL), const),
            pl.BlockSpec((D_MODEL, sum(widths)), const),
            pl.BlockSpec((D_MODEL, LANES), const),
        ],
        out_specs=[pl.BlockSpec((tm, w), row) for w in out_w],
        out_shape=[jax.ShapeDtypeStruct((n, w), F32) for w in out_w],
        compiler_params=pltpu.CompilerParams(
            dimension_semantics=("parallel",), vmem_limit_bytes=VMEM_LIMIT),
        name="inproj",
    )(x2d, n1, w_main, w_dt)


def _ssd_body(xbc_ref, dt_ref, z_ref, cst_ref, sst_ref, cw_ref, cb_ref, dtb_ref, alog_ref,
              dsk_ref, nw_ref, exp_ref, y_ref, sout_ref, cout_ref, xpad, state, *, ch):
    c = pl.program_id(1)

    @pl.when(c == 0)
    def _():
        xpad[0:SUBLANES, :] = cst_ref[0]
        state[...] = sst_ref[0]

    xpad[SUBLANES:SUBLANES + ch, :] = xbc_ref[0]
    conv = cb_ref[...]
    for j in range(D_CONV):
        off = SUBLANES - (D_CONV - 1) + j
        conv = conv + xpad[off:off + ch, :] * cw_ref[j:j + 1, :]
    xa = _silu(conv)
    tail = xpad[ch:ch + SUBLANES, :]
    xpad[0:SUBLANES, :] = tail
    cout_ref[0] = tail

    dtv = _softplus(dt_ref[0] + dtb_ref[...])
    a = -jnp.exp(alog_ref[...])
    da = dtv * a
    rr = lax.broadcasted_iota(jnp.int32, (ch, ch), 0)
    cc = lax.broadcasted_iota(jnp.int32, (ch, ch), 1)
    tri = rr >= cc
    acum = _dot(tri.astype(F32), da, precision=HIGHEST)
    acum_t = acum.T
    expand = exp_ref[...]
    acum_f = _dot(acum, expand, precision=HIGHEST)
    dt_f = _dot(dtv, expand, precision=HIGHEST)

    xs = xa[:, :SSM_WIDTH]
    xdt = xs * dt_f
    to_end = jnp.exp(acum_f[ch - 1:ch, :] - acum_f)
    xdt_te = (xdt * to_end).astype(BF16)
    ea = jnp.exp(acum_f)
    lane_g = lax.broadcasted_iota(jnp.int32, (ch, GROUP_WIDTH), 1) // HEAD_DIM

    ys = []
    for g in range(SSM_GROUPS):
        b_g = xa[:, SSM_WIDTH + g * D_STATE:SSM_WIDTH + (g + 1) * D_STATE].astype(BF16)
        c_off = SSM_WIDTH + SSM_GROUPS * D_STATE
        c_g = xa[:, c_off + g * D_STATE:c_off + (g + 1) * D_STATE].astype(BF16)
        lo, hi = g * GROUP_WIDTH, (g + 1) * GROUP_WIDTH
        scores = _dot_nt(c_g, b_g)
        st_g = state[lo:hi, :]
        y_g = _dot_nt(c_g, st_g.astype(BF16)) * ea[:, lo:hi]
        xdt_g = xdt[:, lo:hi]
        for hh in range(HEADS_PER_GROUP):
            h = g * HEADS_PER_GROUP + hh
            seg = acum[:, h:h + 1] - acum_t[h:h + 1, :]
            decay = jnp.where(tri, jnp.exp(jnp.where(tri, seg, 0.0)), 0.0)
            gm = (scores * decay).astype(BF16)
            rhs = jnp.where(lane_g == hh, xdt_g, 0.0).astype(BF16)
            y_g = y_g + _dot(gm, rhs)
        ys.append(y_g)
        contrib = _dot_tn(xdt_te[:, lo:hi], b_g)
        for hh in range(HEADS_PER_GROUP):
            h = g * HEADS_PER_GROUP + hh
            dec = jnp.exp(acum_t[h:h + 1, ch - 1:ch])
            r0 = h * HEAD_DIM
            state[r0:r0 + HEAD_DIM, :] = (state[r0:r0 + HEAD_DIM, :] * dec
                                          + contrib[hh * HEAD_DIM:(hh + 1) * HEAD_DIM, :])
    y = jnp.concatenate(ys, axis=1) + dsk_ref[...] * xs
    y = y * _silu(z_ref[0])
    outs = []
    for g in range(SSM_GROUPS):
        yg = y[:, g * GROUP_WIDTH:(g + 1) * GROUP_WIDTH]
        outs.append(yg * lax.rsqrt(jnp.mean(yg * yg, axis=-1, keepdims=True) + EPS))
    y_ref[0] = jnp.concatenate(outs, axis=1) * nw_ref[...]
    sout_ref[0] = state[...]


def _ssd(xbc, dt, z, conv_state8, ssm_state, cw8, cb, dtb, alog, dsk, nw, expand, ch):
    b, l, _ = xbc.shape
    blk = lambda bi, ci: (bi, ci, 0)
    per_b = lambda bi, ci: (bi, 0, 0)
    const = lambda bi, ci: (0, 0)
    return pl.pallas_call(
        functools.partial(_ssd_body, ch=ch),
        grid=(b, l // ch),
        in_specs=[
            pl.BlockSpec((1, ch, CONV_DIM), blk),
            pl.BlockSpec((1, ch, LANES), blk),
            pl.BlockSpec((1, ch, SSM_WIDTH), blk),
            pl.BlockSpec((1, SUBLANES, CONV_DIM), per_b),
            pl.BlockSpec((1, SSM_WIDTH, D_STATE), per_b),
            pl.BlockSpec((SUBLANES, CONV_DIM), const),
            pl.BlockSpec((1, CONV_DIM), const),
            pl.BlockSpec((1, LANES), const),
            pl.BlockSpec((1, LANES), const),
            pl.BlockSpec((1, SSM_WIDTH), const),
            pl.BlockSpec((1, SSM_WIDTH), const),
            pl.BlockSpec((LANES, SSM_WIDTH), const),
        ],
        out_specs=[
            pl.BlockSpec((1, ch, SSM_WIDTH), blk),
            pl.BlockSpec((1, SSM_WIDTH, D_STATE), per_b),
            pl.BlockSpec((1, SUBLANES, CONV_DIM), per_b),
        ],
        out_shape=[
            jax.ShapeDtypeStruct((b, l, SSM_WIDTH), F32),
            jax.ShapeDtypeStruct((b, SSM_WIDTH, D_STATE), F32),
            jax.ShapeDtypeStruct((b, SUBLANES, CONV_DIM), F32),
        ],
        scratch_shapes=[
            pltpu.VMEM((ch + SUBLANES, CONV_DIM), F32),
            pltpu.VMEM((SSM_WIDTH, D_STATE), F32),
        ],
        compiler_params=pltpu.CompilerParams(
            dimension_semantics=("parallel", "arbitrary"), vmem_limit_bytes=VMEM_LIMIT),
        name="ssd",
    )(xbc, dt, z, conv_state8, ssm_state, cw8, cb, dtb, alog, dsk, nw, expand)


SB_BLOCK = 128


def _sb_scores(z, mask):
    sp = _softplus(z)
    l1m = -sp
    if mask is not None:
        l1m = jnp.where(mask, l1m, 0.0)
    return l1m, z - sp


def _sbp_body(q_ref, k_ref, v_ref, t_ref, o_ref):
    i = pl.program_id(2)
    nb = SB_BLOCK
    q = q_ref[0] * SB_SCALE
    lane = lax.broadcasted_iota(jnp.int32, (nb, LANES), 1)
    qm = [jnp.where(lane < HEAD_DIM, q, 0.0).astype(BF16),
          jnp.where(lane >= HEAD_DIM, q, 0.0).astype(BF16)]
    tmat = t_ref[...]
    qi = lax.broadcasted_iota(jnp.int32, (nb, nb), 0)
    ki = lax.broadcasted_iota(jnp.int32, (nb, nb), 1)
    diag_mask = ki < qi

    def block(j, carry, mask):
        start = pl.multiple_of(j * nb, nb)
        kj = k_ref[0, pl.ds(start, nb), :].astype(BF16)
        vj = v_ref[0, pl.ds(start, nb), :].astype(BF16)
        out = []
        for hh in range(2):
            acc, cr = carry[2 * hh], carry[2 * hh + 1]
            z = _dot_nt(qm[hh], kj)
            l1m, lb = _sb_scores(z, mask)
            hi, lo = _split_bf16(l1m)
            after = _dot(hi, tmat) + _dot(lo, tmat) + cr
            w = jnp.exp(lb + after)
            if mask is not None:
                w = jnp.where(mask, w, 0.0)
            out.append(acc + _dot(w.astype(BF16), vj))
            out.append(cr + jnp.sum(l1m, axis=1, keepdims=True))
        return tuple(out)

    zero = (jnp.zeros((nb, LANES), F32), jnp.zeros((nb, 1), F32)) * 2
    carry = block(i, zero, diag_mask)
    carry = lax.fori_loop(0, i, lambda jj, cr: block(i - 1 - jj, cr, None), carry)
    o_ref[0] = jnp.where(lane < HEAD_DIM, carry[0], carry[2])


def _sb_prompt(q, k, v, tmat):
    b, l, _ = q.shape
    nb = SB_BLOCK
    return pl.pallas_call(
        _sbp_body,
        grid=(b, SB_WIDTH // LANES, l // nb),
        in_specs=[
            pl.BlockSpec((1, nb, LANES), lambda bi, p, i: (bi, i, p)),
            pl.BlockSpec((1, l, LANES), lambda bi, p, i: (bi, 0, p)),
            pl.BlockSpec((1, l, LANES), lambda bi, p, i: (bi, 0, p)),
            pl.BlockSpec((nb, nb), lambda bi, p, i: (0, 0)),
        ],
        out_specs=pl.BlockSpec((1, nb, LANES), lambda bi, p, i: (bi, i, p)),
        out_shape=jax.ShapeDtypeStruct((b, l, SB_WIDTH), F32),
        compiler_params=pltpu.CompilerParams(
            dimension_semantics=("parallel", "parallel", "arbitrary"),
            vmem_limit_bytes=VMEM_LIMIT),
        name="sb_prompt",
    )(q, k, v, tmat)


def _sbc_body(qbd_ref, kn_ref, vn_ref, kp_ref, vp_ref, tn_ref, tp_ref, o_ref, acc, carry,
              *, lq, kb):
    s = pl.program_id(1)
    cols = SB_HEADS * lq
    qbd = qbd_ref[0]

    def block(kblk, vblk, tmat, mask):
        z = _dot(kblk.astype(BF16), qbd)
        l1m, lb = _sb_scores(z, mask)
        hi, lo = _split_bf16(l1m)
        after = _dot(tmat, hi) + _dot(tmat, lo) + carry[...]
        w = jnp.exp(lb + after)
        if mask is not None:
            w = jnp.where(mask, w, 0.0)
        acc[...] += _dot_tn(w.astype(BF16), vblk.astype(BF16))
        carry[...] += jnp.sum(l1m, axis=0, keepdims=True)

    @pl.when(s == 0)
    def _():
        acc[...] = jnp.zeros_like(acc)
        carry[...] = jnp.zeros_like(carry)
        kpos = lax.broadcasted_iota(jnp.int32, (lq, cols), 0)
        qpos = lax.broadcasted_iota(jnp.int32, (lq, cols), 1) % lq
        block(kn_ref[0], vn_ref[0], tn_ref[...], kpos < qpos)

    block(kp_ref[0], vp_ref[0], tp_ref[...], None)

    @pl.when(s == pl.num_programs(1) - 1)
    def _():
        a3 = acc[...].reshape(SB_HEADS, lq, SB_WIDTH)
        hid = lax.broadcasted_iota(jnp.int32, (SB_HEADS, lq, SB_WIDTH), 0)
        lid = lax.broadcasted_iota(jnp.int32, (SB_HEADS, lq, SB_WIDTH), 2) // HEAD_DIM
        o_ref[0] = jnp.sum(jnp.where(hid == lid, a3, 0.0), axis=0)


def _sb_cached(qbd, k_new, v_new, k_past, v_past, t_new, t_past, kb):
    b, lq, _ = k_new.shape
    past = k_past.shape[1]
    nblk = past // kb
    cols = SB_HEADS * lq
    per_b = lambda bi, s: (bi, 0, 0)
    rev = lambda bi, s: (bi, nblk - 1 - s, 0)
    const = lambda bi, s: (0, 0)
    return pl.pallas_call(
        functools.partial(_sbc_body, lq=lq, kb=kb),
        grid=(b, nblk),
        in_specs=[
            pl.BlockSpec((1, SB_WIDTH, cols), per_b),
            pl.BlockSpec((1, lq, SB_WIDTH), per_b),
            pl.BlockSpec((1, lq, SB_WIDTH), per_b),
            pl.BlockSpec((1, kb, SB_WIDTH), rev),
            pl.BlockSpec((1, kb, SB_WIDTH), rev),
            pl.BlockSpec((lq, lq), const),
            pl.BlockSpec((kb, kb), const),
        ],
        out_specs=pl.BlockSpec((1, lq, SB_WIDTH), per_b),
        out_shape=jax.ShapeDtypeStruct((b, lq, SB_WIDTH), F32),
        scratch_shapes=[pltpu.VMEM((cols, SB_WIDTH), F32), pltpu.VMEM((1, cols), F32)],
        compiler_params=pltpu.CompilerParams(
            dimension_semantics=("parallel", "arbitrary"), vmem_limit_bytes=VMEM_LIMIT),
        name="sb_cached",
    )(qbd, k_new, v_new, k_past, v_past, t_new, t_past)


def _strict_upper(n, dtype):
    a = lax.broadcasted_iota(jnp.int32, (n, n), 0)
    b = lax.broadcasted_iota(jnp.int32, (n, n), 1)
    return (b > a).astype(dtype)


def _prep_mixer_weights(norm1_w, w_in, conv_w, conv_b, dt_bias, a_log, d_skip, ssm_norm_w):
    w_main = jnp.concatenate([w_in[:, :OFF_DT], w_in[:, OFF_Q:]], axis=1).astype(BF16)
    w_dt = jnp.pad(w_in[:, OFF_DT:OFF_Q], ((0, 0), (0, LANES - SSM_HEADS))).astype(BF16)
    pad_h = lambda t: jnp.pad(t, (0, LANES - SSM_HEADS)).reshape(1, LANES)
    head_of_lane = jnp.arange(SSM_WIDTH) // HEAD_DIM
    expand = (jnp.arange(LANES)[:, None] == head_of_lane[None, :]).astype(F32)
    return dict(
        n1=norm1_w.reshape(1, D_MODEL), w_main=w_main, w_dt=w_dt,
        cw8=jnp.pad(conv_w, ((0, SUBLANES - D_CONV), (0, 0))), cb=conv_b.reshape(1, CONV_DIM),
        dtb=pad_h(dt_bias), alog=pad_h(a_log), dsk=jnp.repeat(d_skip, HEAD_DIM).reshape(1, SSM_WIDTH),
        nw=ssm_norm_w.reshape(1, SSM_WIDTH), expand=expand)


def _mixer(x, conv_state, ssm_state, k_past, v_past, mw, ch, tm):
    b, l, _ = x.shape
    z, xbc, dt, q, k, v = _inproj(x.reshape(b * l, D_MODEL), mw["n1"], mw["w_main"], mw["w_dt"], tm)
    r3 = lambda t: t.reshape(b, l, t.shape[-1])
    z, xbc, dt, q, k, v = map(r3, (z, xbc, dt, q, k, v))
    cst8 = jnp.pad(conv_state, ((0, 0), (SUBLANES - (D_CONV - 1), 0), (0, 0)))
    y_ssm, new_ssm, cout8 = _ssd(xbc, dt, z, cst8, ssm_state.reshape(b, SSM_WIDTH, D_STATE),
                                 mw["cw8"], mw["cb"], mw["dtb"], mw["alog"], mw["dsk"], mw["nw"],
                                 mw["expand"], ch)
    new_conv = cout8[:, SUBLANES - (D_CONV - 1):, :]
    if k_past is None:
        o_sb = _sb_prompt(q, k, v, _strict_upper(SB_BLOCK, BF16).T)
    else:
        past = k_past.shape[1]
        kb = min(512, past)
        qs = (q * SB_SCALE).reshape(b, l, SB_HEADS, HEAD_DIM)
        eye = jnp.eye(SB_HEADS, dtype=F32)
        qbd = jnp.einsum("bqhd,hg->bhdgq", qs, eye).reshape(b, SB_WIDTH, SB_HEADS * l).astype(BF16)
        o_sb = _sb_cached(qbd, k, v, k_past.reshape(b, past, SB_WIDTH), v_past.reshape(b, past, SB_WIDTH),
                          _strict_upper(l, BF16), _strict_upper(kb, BF16), kb)
    return y_ssm, o_sb, k, v, new_ssm, new_conv


NEG_INF = float("-inf")
_CAND_PIECES = (
    ((0, 16), (0, 1), None),
    ((0, 8), (1, 2), None),
    ((0, 1), (0, 16), (2, 15)),
    ((1, 2), (0, 8), (2, 7)),
    ((2, 3), (0, 8), (2, 4)),
    ((3, 4), (0, 8), (2, 3)),
    ((4, 5), (0, 8), (2, 2)),
)


def _topk_rows(s, key, k):
    t = s.shape[1]
    rid = lax.broadcasted_iota(jnp.int32, (k, t), 0)
    vals = jnp.zeros((k, t), F32)
    keys = jnp.zeros((k, t), jnp.int32)
    big = jnp.int32(2 ** 30)
    for r in range(k):
        m = jnp.max(s, axis=0, keepdims=True)
        sel_key = jnp.min(jnp.where(s == m, key, big), axis=0, keepdims=True)
        vals = jnp.where(rid == r, m, vals)
        keys = jnp.where(rid == r, sel_key, keys)
        s = jnp.where(key == sel_key, NEG_INF, s)
    return vals, keys


def _route_body(x_ref, ys_ref, os_ref, wo_ref, n2_ref, wqt_ref, k1_ref, k2_ref,
                h_ref, xn_ref, eid_ref, gate_ref):
    tt = x_ref.shape[0]
    h = (x_ref[...] + _dot(ys_ref[...].astype(BF16), wo_ref[0:SSM_WIDTH, :])
         + _dot(os_ref[...].astype(BF16), wo_ref[SSM_WIDTH:, :]))
    h_ref[...] = h
    xn = h * lax.rsqrt(jnp.mean(h * h, axis=-1, keepdims=True) + EPS) * n2_ref[...]
    xn_ref[...] = xn
    qt = _dot_nt(wqt_ref[...], xn.astype(BF16))

    kid = lax.broadcasted_iota(jnp.int32, (N_KEYS, tt), 0)
    eids, gates = [], []
    for hd in range(PEER_HEADS):
        q1 = qt[hd * D_KEY:hd * D_KEY + HALF_KEY, :].astype(BF16)
        q2 = qt[hd * D_KEY + HALF_KEY:(hd + 1) * D_KEY, :].astype(BF16)
        v1, i1 = _topk_rows(_dot(k1_ref[hd], q1), kid, PEER_TOPK)
        v2, i2 = _topk_rows(_dot(k2_ref[hd], q2), kid, PEER_TOPK)
        cand, cexp, cflat = [], [], []
        for (a0, a1), (b0, b1), valid in _CAND_PIECES:
            n = max(a1 - a0, b1 - b0)
            c = v1[a0:a1, :] + v2[b0:b1, :]
            e = i1[a0:a1, :] * N_KEYS + i2[b0:b1, :]
            ai = a0 + (lax.broadcasted_iota(jnp.int32, (n, tt), 0) if a1 - a0 > 1 else 0)
            bi = b0 + (lax.broadcasted_iota(jnp.int32, (n, tt), 0) if b1 - b0 > 1 else 0)
            f = ai * PEER_TOPK + bi + jnp.zeros((n, tt), jnp.int32)
            if valid is not None:
                c = jnp.where((bi >= valid[0]) & (bi <= valid[1]), c, NEG_INF)
            cand.append(c)
            cexp.append(e)
            cflat.append(f)
        cand = jnp.concatenate(cand, axis=0)
        cexp = jnp.concatenate(cexp, axis=0)
        cflat = jnp.concatenate(cflat, axis=0)
        best, bflat = _topk_rows(cand, cflat, PEER_TOPK)
        rid = lax.broadcasted_iota(jnp.int32, (PEER_TOPK, tt), 0)
        bexp = jnp.zeros((PEER_TOPK, tt), jnp.int32)
        for r in range(PEER_TOPK):
            e_r = jnp.max(jnp.where(cflat == bflat[r:r + 1, :], cexp, -1), axis=0, keepdims=True)
            bexp = jnp.where(rid == r, e_r, bexp)
        p = jnp.exp(best - best[0:1, :])
        gates.append(p / jnp.sum(p, axis=0, keepdims=True))
        eids.append(bexp)
    eid_ref[...] = jnp.concatenate(eids, axis=0).T
    gate_ref[...] = jnp.concatenate(gates, axis=0).T


def _route(x2d, ys2d, os2d, w_out, n2, wqt, k1, k2, tt):
    n = x2d.shape[0]
    items = PEER_HEADS * PEER_TOPK
    row = lambda i: (i, 0)
    const2 = lambda i: (0, 0)
    const3 = lambda i: (0, 0, 0)
    return pl.pallas_call(
        _route_body,
        grid=(n // tt,),
        in_specs=[
            pl.BlockSpec((tt, D_MODEL), row),
            pl.BlockSpec((tt, SSM_WIDTH), row),
            pl.BlockSpec((tt, SB_WIDTH), row),
            pl.BlockSpec((SSM_WIDTH + SB_WIDTH, D_MODEL), const2),
            pl.BlockSpec((1, D_MODEL), const2),
            pl.BlockSpec((PEER_HEADS * D_KEY, D_MODEL), const2),
            pl.BlockSpec((PEER_HEADS, N_KEYS, HALF_KEY), const3),
            pl.BlockSpec((PEER_HEADS, N_KEYS, HALF_KEY), const3),
        ],
        out_specs=[
            pl.BlockSpec((tt, D_MODEL), row),
            pl.BlockSpec((tt, D_MODEL), row),
            pl.BlockSpec((tt, items), row),
            pl.BlockSpec((tt, items), row),
        ],
        out_shape=[
            jax.ShapeDtypeStruct((n, D_MODEL), F32),
            jax.ShapeDtypeStruct((n, D_MODEL), F32),
            jax.ShapeDtypeStruct((n, items), jnp.int32),
            jax.ShapeDtypeStruct((n, items), F32),
        ],
        compiler_params=pltpu.CompilerParams(
            dimension_semantics=("parallel",), vmem_limit_bytes=VMEM_LIMIT),
        name="peer_route",
    )(x2d, ys2d, os2d, w_out, n2, wqt, k1, k2)


PEER_ITEMS = PEER_HEADS * PEER_TOPK
ROW_CHUNKS = D_MODEL // (2 * LANES)
GROUP = SUBLANES


def _pack_table(tab):
    t = tab.astype(BF16).reshape(tab.shape[0], ROW_CHUNKS, 2, LANES)
    bits = lax.bitcast_convert_type(t, jnp.uint16).astype(jnp.uint32)
    words = bits[:, :, 0, :] | (bits[:, :, 1, :] << 16)
    return lax.bitcast_convert_type(words, jnp.int32).reshape(tab.shape[0] * ROW_CHUNKS, LANES)


def _unpack_words(w):
    lo = lax.bitcast_convert_type(lax.shift_left(w, 16), F32).astype(BF16)
    hi = lax.bitcast_convert_type(w & jnp.int32(-65536), F32).astype(BF16)
    return lo, hi


def _gather_rows(eid_ref, tab_ref, gbuf, tb, stride):
    def tok(t, carry):
        for i in range(PEER_ITEMS):
            src = pl.multiple_of(eid_ref[t, i] * ROW_CHUNKS, ROW_CHUNKS)
            slab = tab_ref[pl.ds(src, ROW_CHUNKS), :]
            gbuf[pl.ds(t * PEER_ITEMS + i, ROW_CHUNKS, stride=stride), :] = slab
        return carry
    lax.fori_loop(0, tb, tok, 0)


def _pick_row(acc, t):
    rid = lax.broadcasted_iota(jnp.int32, acc.shape, 0)
    return jnp.sum(jnp.where((rid % GROUP) == t, acc, 0.0), axis=0, keepdims=True)


def _peer_u_body(eid_ref, x_ref, tab_ref, act_ref, gbuf, *, tb, stride):
    _gather_rows(eid_ref, tab_ref, gbuf, tb, stride)

    def group(g, carry):
        r0 = pl.multiple_of(g * GROUP, GROUP)
        xg = x_ref[pl.ds(r0, GROUP), :]
        xh, xl = _split_bf16(xg)
        x16 = jnp.concatenate([xh, xl], axis=0)
        for t in range(GROUP):
            acc = jnp.zeros((2 * GROUP, PEER_ITEMS), F32)
            for c in range(ROW_CHUNKS):
                start = pl.multiple_of(c * stride + (r0 + t) * PEER_ITEMS, PEER_ITEMS)
                lo, hi = _unpack_words(gbuf[pl.ds(start, PEER_ITEMS), :])
                acc = acc + _dot_nt(x16[:, 2 * c * LANES:(2 * c + 1) * LANES], lo)
                acc = acc + _dot_nt(x16[:, (2 * c + 1) * LANES:(2 * c + 2) * LANES], hi)
            act_ref[pl.ds(r0 + t, 1), :] = _pick_row(acc, t)
        return carry
    lax.fori_loop(0, tb // GROUP, group, 0)


def _gelu_exact(x):
    return 0.5 * x * (1.0 + lax.erf(x * (2.0 ** -0.5)))


def _peer_v_body(eid_ref, act_ref, gate_ref, h_ref, fnw_ref, tab_ref, y_ref, gbuf, *, tb, stride):
    _gather_rows(eid_ref, tab_ref, gbuf, tb, stride)

    def group(g, carry):
        r0 = pl.multiple_of(g * GROUP, GROUP)
        coef = gate_ref[pl.ds(r0, GROUP), :] * _gelu_exact(act_ref[pl.ds(r0, GROUP), :])
        ch, cl = _split_bf16(coef)
        c16 = jnp.concatenate([ch, cl], axis=0)
        for t in range(GROUP):
            parts = []
            for c in range(ROW_CHUNKS):
                start = pl.multiple_of(c * stride + (r0 + t) * PEER_ITEMS, PEER_ITEMS)
                lo, hi = _unpack_words(gbuf[pl.ds(start, PEER_ITEMS), :])
                parts.append(_pick_row(_dot(c16, lo), t))
                parts.append(_pick_row(_dot(c16, hi), t))
            peer = jnp.concatenate(parts, axis=1)
            y = h_ref[pl.ds(r0 + t, 1), :] + peer
            y_ref[pl.ds(r0 + t, 1), :] = (y * lax.rsqrt(jnp.mean(y * y, axis=-1, keepdims=True) + EPS)
                                          * fnw_ref[...])
        return carry
    lax.fori_loop(0, tb // GROUP, group, 0)


def _peer_specs(n, tb):
    stride = tb * PEER_ITEMS + SUBLANES
    row = lambda i: (i, 0)
    const = lambda i: (0, 0)
    eid_spec = pl.BlockSpec((tb, PEER_ITEMS), row, memory_space=pltpu.SMEM)
    tab_spec = pl.BlockSpec((N_KEYS * N_KEYS * ROW_CHUNKS, LANES), const, pipeline_mode=pl.Buffered(1))
    gbuf = pltpu.VMEM((ROW_CHUNKS * stride, LANES), jnp.int32)
    params = pltpu.CompilerParams(dimension_semantics=("arbitrary",), vmem_limit_bytes=VMEM_LIMIT)
    return stride, row, const, eid_spec, tab_spec, gbuf, params


def _peer_u(eid, xn, u_packed, tb):
    n = xn.shape[0]
    stride, row, const, eid_spec, tab_spec, gbuf, params = _peer_specs(n, tb)
    return pl.pallas_call(
        functools.partial(_peer_u_body, tb=tb, stride=stride),
        grid=(n // tb,),
        in_specs=[eid_spec, pl.BlockSpec((tb, D_MODEL), row), tab_spec],
        out_specs=pl.BlockSpec((tb, PEER_ITEMS), row),
        out_shape=jax.ShapeDtypeStruct((n, PEER_ITEMS), F32),
        scratch_shapes=[gbuf],
        compiler_params=params,
        name="peer_u",
    )(eid, xn, u_packed)


def _peer_v(eid, act, gate, h, fnw, v_packed, tb):
    n = h.shape[0]
    stride, row, const, eid_spec, tab_spec, gbuf, params = _peer_specs(n, tb)
    return pl.pallas_call(
        functools.partial(_peer_v_body, tb=tb, stride=stride),
        grid=(n // tb,),
        in_specs=[eid_spec, pl.BlockSpec((tb, PEER_ITEMS), row), pl.BlockSpec((tb, PEER_ITEMS), row),
                  pl.BlockSpec((tb, D_MODEL), row), pl.BlockSpec((1, D_MODEL), const), tab_spec],
        out_specs=pl.BlockSpec((tb, D_MODEL), row),
        out_shape=jax.ShapeDtypeStruct((n, D_MODEL), F32),
        scratch_shapes=[gbuf],
        compiler_params=params,
        name="peer_v",
    )(eid, act, gate, h, fnw, v_packed)


def _channel_mixer(x2d, ys2d, os2d, cw, tt, tb):
    h, xn, eid, gate = _route(x2d, ys2d, os2d, cw["w_out"], cw["n2"], cw["wqt"], cw["k1"], cw["k2"], tt)
    act = _peer_u(eid, xn, cw["u"], tb)
    return _peer_v(eid, act, gate, h, cw["fnw"], cw["v"], tb)


def kernel(x_prompt, x_sample, cache_sb_k, cache_sb_v, state_ssm, state_conv, norm1_w, w_in, conv_w,
           conv_b, dt_bias, a_log, d_skip, ssm_norm_w, w_out, norm2_w, peer_wq, peer_k1, peer_k2,
           peer_u, peer_v, final_norm_w):
    assert norm1_w.shape[0] == 1, "single layer"
    bp, lp, _ = x_prompt.shape
    bs, ls, _ = x_sample.shape
    mw = _prep_mixer_weights(norm1_w[0], w_in[0], conv_w[0], conv_b[0], dt_bias[0], a_log[0],
                             d_skip[0], ssm_norm_w[0])
    cw = dict(w_out=w_out[0].astype(BF16), n2=norm2_w[0].reshape(1, D_MODEL),
              wqt=peer_wq[0].T.astype(BF16), k1=peer_k1[0].astype(BF16), k2=peer_k2[0].astype(BF16),
              u=_pack_table(peer_u[0]), v=_pack_table(peer_v[0]), fnw=final_norm_w.reshape(1, D_MODEL))

    conv0 = jnp.zeros((bp, D_CONV - 1, CONV_DIM), F32)
    ssm0 = jnp.zeros((bp, SSM_HEADS, HEAD_DIM, D_STATE), F32)
    ysp, osp, kp, vp, sp, cp = _mixer(x_prompt, conv0, ssm0, None, None, mw, min(256, lp), min(512, bp * lp))
    yss, oss, ks, vs, ss, cs = _mixer(x_sample, state_conv[0], state_ssm[0], cache_sb_k[0], cache_sb_v[0],
                                      mw, ls, min(512, bs * ls))
    npr, nsa = bp * lp, bs * ls
    x_all = jnp.concatenate([x_prompt.reshape(npr, D_MODEL), x_sample.reshape(nsa, D_MODEL)], axis=0)
    ys_all = jnp.concatenate([ysp.reshape(npr, SSM_WIDTH), yss.reshape(nsa, SSM_WIDTH)], axis=0)
    os_all = jnp.concatenate([osp.reshape(npr, SB_WIDTH), oss.reshape(nsa, SB_WIDTH)], axis=0)
    y_all = _channel_mixer(x_all, ys_all, os_all, cw, 256, 32)
    y_prompt = y_all[:npr].reshape(bp, lp, D_MODEL)
    y_sample = y_all[npr:].reshape(bs, ls, D_MODEL)
    hd = lambda t, b, l: t.reshape(1, b, l, SB_HEADS, HEAD_DIM)
    return (y_prompt, y_sample, hd(kp, bp, lp), hd(vp, bp, lp),
            sp.reshape(1, bp, SSM_HEADS, HEAD_DIM, D_STATE), cp[None],
            hd(ks, bs, ls), hd(vs, bs, ls),
            ss.reshape(1, bs, SSM_HEADS, HEAD_DIM, D_STATE), cs[None])
```

```python
import functools

import jax
import jax.numpy as jnp
from jax import lax
from jax.experimental import pallas as pl
from jax.experimental.pallas import tpu as pltpu

F32 = jnp.float32
BF16 = jnp.bfloat16
HIGHEST = lax.Precision.HIGHEST

D_MODEL = 1024
HEAD_DIM = 64
SSM_HEADS = 8
SSM_WIDTH = SSM_HEADS * HEAD_DIM
SSM_GROUPS = 2
GROUP_WIDTH = SSM_WIDTH // SSM_GROUPS
HEADS_PER_GROUP = SSM_HEADS // SSM_GROUPS
D_STATE = 128
D_CONV = 4
CONV_DIM = SSM_WIDTH + 2 * SSM_GROUPS * D_STATE
SB_HEADS = 8
SB_WIDTH = SB_HEADS * HEAD_DIM
SB_SCALE = HEAD_DIM ** -0.5
OFF_XBC = SSM_WIDTH
OFF_DT = OFF_XBC + CONV_DIM
OFF_Q = OFF_DT + SSM_HEADS
PEER_HEADS = 8
N_KEYS = 128
D_KEY = 256
HALF_KEY = D_KEY // 2
PEER_TOPK = 16
EPS = 1e-6

LANES = 128
SUBLANES = 8
PEER_ITEMS = PEER_HEADS * PEER_TOPK
ROW_CHUNKS = D_MODEL // (2 * LANES)
VMEM_LIMIT = 56 * 1024 * 1024


def _softplus(x):
    return jnp.maximum(x, 0.0) + jnp.log(1.0 + jnp.exp(-jnp.abs(x)))


def _silu(x):
    return x * (1.0 / (1.0 + jnp.exp(-x)))


def _dot(a, b, **kw):
    return jnp.dot(a, b, preferred_element_type=F32, **kw)


def _dot_nt(a, b):
    return lax.dot_general(a, b, (((1,), (1,)), ((), ())), preferred_element_type=F32)


def _dot_tn(a, b):
    return lax.dot_general(a, b, (((0,), (0,)), ((), ())), preferred_element_type=F32)


def _split_bf16(x):
    hi = x.astype(BF16)
    lo = (x - hi.astype(F32)).astype(BF16)
    return hi, lo


def _inproj_body(x_ref, n1_ref, w_ref, wdt_ref, z_ref, xbc_ref, dt_ref, q_ref, k_ref, v_ref):
    x = x_ref[...]
    xn = x * lax.rsqrt(jnp.mean(x * x, axis=-1, keepdims=True) + EPS) * n1_ref[...]
    xb = xn.astype(BF16)
    o = 0
    for ref in (z_ref, xbc_ref, q_ref, k_ref, v_ref):
        wd = ref.shape[-1]
        ref[...] = _dot(xb, w_ref[:, o:o + wd])
        o += wd
    dt_ref[...] = _dot(xb, wdt_ref[...])


def _inproj(x2d, n1, w_main, w_dt, tm):
    n = x2d.shape[0]
    widths = (SSM_WIDTH, CONV_DIM, SB_WIDTH, SB_WIDTH, SB_WIDTH)
    out_w = (SSM_WIDTH, CONV_DIM, LANES, SB_WIDTH, SB_WIDTH, SB_WIDTH)
    row = lambda i: (i, 0)
    const = lambda i: (0, 0)
    return pl.pallas_call(
        _inproj_body,
        grid=(n // tm,),
        in_specs=[
            pl.BlockSpec((tm, D_MODEL), row),
            pl.BlockSpec((1, D_MODEL), const),
            pl.BlockSpec((D_MODEL, sum(widths)), const),
            pl.BlockSpec((D_MODEL, LANES), const),
        ],
        out_specs=[pl.BlockSpec((tm, w), row) for w in out_w],
        out_shape=[jax.ShapeDtypeStruct((n, w), F32) for w in out_w],
        compiler_params=pltpu.CompilerParams(
            dimension_semantics=("parallel",), vmem_limit_bytes=VMEM_LIMIT),
        name="inproj",
    )(x2d, n1, w_main, w_dt)


def _ssd_body(xbc_ref, dt_ref, z_ref, cst_ref, sst_ref, cw_ref, cb_ref, dtb_ref, alog_ref,
              dsk_ref, nw_ref, exp_ref, y_ref, sout_ref, cout_ref, xpad, state, *, ch):
    c = pl.program_id(1)

    @pl.when(c == 0)
    def _():
        xpad[0:SUBLANES, :] = cst_ref[0]
        state[...] = sst_ref[0]

    xpad[SUBLANES:SUBLANES + ch, :] = xbc_ref[0]
    conv = cb_ref[...]
    for j in range(D_CONV):
        off = SUBLANES - (D_CONV - 1) + j
        conv = conv + xpad[off:off + ch, :] * cw_ref[j:j + 1, :]
    xa = _silu(conv)
    tail = xpad[ch:ch + SUBLANES, :]
    xpad[0:SUBLANES, :] = tail
    cout_ref[0] = tail

    dtv = _softplus(dt_ref[0] + dtb_ref[...])
    a = -jnp.exp(alog_ref[...])
    da = dtv * a
    rr = lax.broadcasted_iota(jnp.int32, (ch, ch), 0)
    cc = lax.broadcasted_iota(jnp.int32, (ch, ch), 1)
    tri = rr >= cc
    acum = _dot(tri.astype(F32), da, precision=HIGHEST)
    acum_t = acum.T
    expand = exp_ref[...]
    acum_f = _dot(acum, expand, precision=HIGHEST)
    dt_f = _dot(dtv, expand, precision=HIGHEST)

    xs = xa[:, :SSM_WIDTH]
    xdt = xs * dt_f
    to_end = jnp.exp(acum_f[ch - 1:ch, :] - acum_f)
    xdt_te = (xdt * to_end).astype(BF16)
    ea = jnp.exp(acum_f)
    lane_g = lax.broadcasted_iota(jnp.int32, (ch, GROUP_WIDTH), 1) // HEAD_DIM

    ys = []
    for g in range(SSM_GROUPS):
        b_g = xa[:, SSM_WIDTH + g * D_STATE:SSM_WIDTH + (g + 1) * D_STATE].astype(BF16)
        c_off = SSM_WIDTH + SSM_GROUPS * D_STATE
        c_g = xa[:, c_off + g * D_STATE:c_off + (g + 1) * D_STATE].astype(BF16)
        lo, hi = g * GROUP_WIDTH, (g + 1) * GROUP_WIDTH
        scores = _dot_nt(c_g, b_g)
        st_g = state[lo:hi, :]
        y_g = _dot_nt(c_g, st_g.astype(BF16)) * ea[:, lo:hi]
        xdt_g = xdt[:, lo:hi]
        for hh in range(HEADS_PER_GROUP):
            h = g * HEADS_PER_GROUP + hh
            seg = acum[:, h:h + 1] - acum_t[h:h + 1, :]
            decay = jnp.where(tri, jnp.exp(jnp.where(tri, seg, 0.0)), 0.0)
            gm = (scores * decay).astype(BF16)
            rhs = jnp.where(lane_g == hh, xdt_g, 0.0).astype(BF16)
            y_g = y_g + _dot(gm, rhs)
        ys.append(y_g)
        contrib = _dot_tn(xdt_te[:, lo:hi], b_g)
        for hh in range(HEADS_PER_GROUP):
            h = g * HEADS_PER_GROUP + hh
            dec = jnp.exp(acum_t[h:h + 1, ch - 1:ch])
            r0 = h * HEAD_DIM
            state[r0:r0 + HEAD_DIM, :] = (state[r0:r0 + HEAD_DIM, :] * dec
                                          + contrib[hh * HEAD_DIM:(hh + 1) * HEAD_DIM, :])
    y = jnp.concatenate(ys, axis=1) + dsk_ref[...] * xs
    y = y * _silu(z_ref[0])
    outs = []
    for g in range(SSM_GROUPS):
        yg = y[:, g * GROUP_WIDTH:(g + 1) * GROUP_WIDTH]
        outs.append(yg * lax.rsqrt(jnp.mean(yg * yg, axis=-1, keepdims=True) + EPS))
    y_ref[0] = jnp.concatenate(outs, axis=1) * nw_ref[...]
    sout_ref[0] = state[...]


def _ssd(xbc, dt, z, conv_state8, ssm_state, cw8, cb, dtb, alog, dsk, nw, expand, ch):
    b, l, _ = xbc.shape
    blk = lambda bi, ci: (bi, ci, 0)
    per_b = lambda bi, ci: (bi, 0, 0)
    const = lambda bi, ci: (0, 0)
    return pl.pallas_call(
        functools.partial(_ssd_body, ch=ch),
        grid=(b, l // ch),
        in_specs=[
            pl.BlockSpec((1, ch, CONV_DIM), blk),
            pl.BlockSpec((1, ch, LANES), blk),
            pl.BlockSpec((1, ch, SSM_WIDTH), blk),
            pl.BlockSpec((1, SUBLANES, CONV_DIM), per_b),
            pl.BlockSpec((1, SSM_WIDTH, D_STATE), per_b),
            pl.BlockSpec((SUBLANES, CONV_DIM), const),
            pl.BlockSpec((1, CONV_DIM), const),
            pl.BlockSpec((1, LANES), const),
            pl.BlockSpec((1, LANES), const),
            pl.BlockSpec((1, SSM_WIDTH), const),
            pl.BlockSpec((1, SSM_WIDTH), const),
            pl.BlockSpec((LANES, SSM_WIDTH), const),
        ],
        out_specs=[
            pl.BlockSpec((1, ch, SSM_WIDTH), blk),
            pl.BlockSpec((1, SSM_WIDTH, D_STATE), per_b),
            pl.BlockSpec((1, SUBLANES, CONV_DIM), per_b),
        ],
        out_shape=[
            jax.ShapeDtypeStruct((b, l, SSM_WIDTH), F32),
            jax.ShapeDtypeStruct((b, SSM_WIDTH, D_STATE), F32),
            jax.ShapeDtypeStruct((b, SUBLANES, CONV_DIM), F32),
        ],
        scratch_shapes=[
            pltpu.VMEM((ch + SUBLANES, CONV_DIM), F32),
            pltpu.VMEM((SSM_WIDTH, D_STATE), F32),
        ],
        compiler_params=pltpu.CompilerParams(
            dimension_semantics=("parallel", "arbitrary"), vmem_limit_bytes=VMEM_LIMIT),
        name="ssd",
    )(xbc, dt, z, conv_state8, ssm_state, cw8, cb, dtb, alog, dsk, nw, expand)


SB_BLOCK = 128


def _sb_scores(z, mask):
    sp = _softplus(z)
    l1m = -sp
    if mask is not None:
        l1m = jnp.where(mask, l1m, 0.0)
    return l1m, z - sp


def _sbp_body(q_ref, k_ref, v_ref, t_ref, o_ref):
    i = pl.program_id(2)
    nb = SB_BLOCK
    q = q_ref[0] * SB_SCALE
    lane = lax.broadcasted_iota(jnp.int32, (nb, LANES), 1)
    qm = [jnp.where(lane < HEAD_DIM, q, 0.0).astype(BF16),
          jnp.where(lane >= HEAD_DIM, q, 0.0).astype(BF16)]
    tmat = t_ref[...]
    qi = lax.broadcasted_iota(jnp.int32, (nb, nb), 0)
    ki = lax.broadcasted_iota(jnp.int32, (nb, nb), 1)
    diag_mask = ki < qi

    def block(j, carry, mask):
        start = pl.multiple_of(j * nb, nb)
        kj = k_ref[0, pl.ds(start, nb), :].astype(BF16)
        vj = v_ref[0, pl.ds(start, nb), :].astype(BF16)
        out = []
        for hh in range(2):
            acc, cr = carry[2 * hh], carry[2 * hh + 1]
            z = _dot_nt(qm[hh], kj)
            l1m, lb = _sb_scores(z, mask)
            hi, lo = _split_bf16(l1m)
            after = _dot(hi, tmat) + _dot(lo, tmat) + cr
            w = jnp.exp(lb + after)
            if mask is not None:
                w = jnp.where(mask, w, 0.0)
            out.append(acc + _dot(w.astype(BF16), vj))
            out.append(cr + jnp.sum(l1m, axis=1, keepdims=True))
        return tuple(out)

    zero = (jnp.zeros((nb, LANES), F32), jnp.zeros((nb, 1), F32)) * 2
    carry = block(i, zero, diag_mask)
    carry = lax.fori_loop(0, i, lambda jj, cr: block(i - 1 - jj, cr, None), carry)
    o_ref[0] = jnp.where(lane < HEAD_DIM, carry[0], carry[2])


def _sb_prompt(q, k, v, tmat):
    b, l, _ = q.shape
    nb = SB_BLOCK
    return pl.pallas_call(
        _sbp_body,
        grid=(b, SB_WIDTH // LANES, l // nb),
        in_specs=[
            pl.BlockSpec((1, nb, LANES), lambda bi, p, i: (bi, i, p)),
            pl.BlockSpec((1, l, LANES), lambda bi, p, i: (bi, 0, p)),
            pl.BlockSpec((1, l, LANES), lambda bi, p, i: (bi, 0, p)),
            pl.BlockSpec((nb, nb), lambda bi, p, i: (0, 0)),
        ],
        out_specs=pl.BlockSpec((1, nb, LANES), lambda bi, p, i: (bi, i, p)),
        out_shape=jax.ShapeDtypeStruct((b, l, SB_WIDTH), F32),
        compiler_params=pltpu.CompilerParams(
            dimension_semantics=("parallel", "parallel", "arbitrary"),
            vmem_limit_bytes=VMEM_LIMIT),
        name="sb_prompt",
    )(q, k, v, tmat)


def _sbc_body(qbd_ref, kn_ref, vn_ref, kp_ref, vp_ref, tn_ref, tp_ref, o_ref, acc, carry,
              *, lq, kb):
    s = pl.program_id(1)
    cols = SB_HEADS * lq
    qbd = qbd_ref[0]

    def block(kblk, vblk, tmat, mask):
        z = _dot(kblk.astype(BF16), qbd)
        l1m, lb = _sb_scores(z, mask)
        hi, lo = _split_bf16(l1m)
        after = _dot(tmat, hi) + _dot(tmat, lo) + carry[...]
        w = jnp.exp(lb + after)
        if mask is not None:
            w = jnp.where(mask, w, 0.0)
        acc[...] += _dot_tn(w.astype(BF16), vblk.astype(BF16))
        carry[...] += jnp.sum(l1m, axis=0, keepdims=True)

    @pl.when(s == 0)
    def _():
        acc[...] = jnp.zeros_like(acc)
        carry[...] = jnp.zeros_like(carry)
        kpos = lax.broadcasted_iota(jnp.int32, (lq, cols), 0)
        qpos = lax.broadcasted_iota(jnp.int32, (lq, cols), 1) % lq
        block(kn_ref[0], vn_ref[0], tn_ref[...], kpos < qpos)

    block(kp_ref[0], vp_ref[0], tp_ref[...], None)

    @pl.when(s == pl.num_programs(1) - 1)
    def _():
        a3 = acc[...].reshape(SB_HEADS, lq, SB_WIDTH)
        hid = lax.broadcasted_iota(jnp.int32, (SB_HEADS, lq, SB_WIDTH), 0)
        lid = lax.broadcasted_iota(jnp.int32, (SB_HEADS, lq, SB_WIDTH), 2) // HEAD_DIM
        o_ref[0] = jnp.sum(jnp.where(hid == lid, a3, 0.0), axis=0)


def _sb_cached(qbd, k_new, v_new, k_past, v_past, t_new, t_past, kb):
    b, lq, _ = k_new.shape
    past = k_past.shape[1]
    nblk = past // kb
    cols = SB_HEADS * lq
    per_b = lambda bi, s: (bi, 0, 0)
    rev = lambda bi, s: (bi, nblk - 1 - s, 0)
    const = lambda bi, s: (0, 0)
    return pl.pallas_call(
        functools.partial(_sbc_body, lq=lq, kb=kb),
        grid=(b, nblk),
        in_specs=[
            pl.BlockSpec((1, SB_WIDTH, cols), per_b),
            pl.BlockSpec((1, lq, SB_WIDTH), per_b),
            pl.BlockSpec((1, lq, SB_WIDTH), per_b),
            pl.BlockSpec((1, kb, SB_WIDTH), rev),
            pl.BlockSpec((1, kb, SB_WIDTH), rev),
            pl.BlockSpec((lq, lq), const),
            pl.BlockSpec((kb, kb), const),
        ],
        out_specs=pl.BlockSpec((1, lq, SB_WIDTH), per_b),
        out_shape=jax.ShapeDtypeStruct((b, lq, SB_WIDTH), F32),
        scratch_shapes=[pltpu.VMEM((cols, SB_WIDTH), F32), pltpu.VMEM((1, cols), F32)],
        compiler_params=pltpu.CompilerParams(
            dimension_semantics=("parallel", "arbitrary"), vmem_limit_bytes=VMEM_LIMIT),
        name="sb_cached",
    )(qbd, k_new, v_new, k_past, v_past, t_new, t_past)


def _strict_upper(n, dtype):
    a = lax.broadcasted_iota(jnp.int32, (n, n), 0)
    b = lax.broadcasted_iota(jnp.int32, (n, n), 1)
    return (b > a).astype(dtype)


def _prep_mixer_weights(norm1_w, w_in, conv_w, conv_b, dt_bias, a_log, d_skip, ssm_norm_w):
    w_main = jnp.concatenate([w_in[:, :OFF_DT], w_in[:, OFF_Q:]], axis=1).astype(BF16)
    w_dt = jnp.pad(w_in[:, OFF_DT:OFF_Q], ((0, 0), (0, LANES - SSM_HEADS))).astype(BF16)
    pad_h = lambda t: jnp.pad(t, (0, LANES - SSM_HEADS)).reshape(1, LANES)
    head_of_lane = jnp.arange(SSM_WIDTH) // HEAD_DIM
    expand = (jnp.arange(LANES)[:, None] == head_of_lane[None, :]).astype(F32)
    return dict(
        n1=norm1_w.reshape(1, D_MODEL), w_main=w_main, w_dt=w_dt,
        cw8=jnp.pad(conv_w, ((0, SUBLANES - D_CONV), (0, 0))), cb=conv_b.reshape(1, CONV_DIM),
        dtb=pad_h(dt_bias), alog=pad_h(a_log), dsk=jnp.repeat(d_skip, HEAD_DIM).reshape(1, SSM_WIDTH),
        nw=ssm_norm_w.reshape(1, SSM_WIDTH), expand=expand)


def _mixer(x, conv_state, ssm_state, k_past, v_past, mw, ch, tm):
    b, l, _ = x.shape
    z, xbc, dt, q, k, v = _inproj(x.reshape(b * l, D_MODEL), mw["n1"], mw["w_main"], mw["w_dt"], tm)
    r3 = lambda t: t.reshape(b, l, t.shape[-1])
    z, xbc, dt, q, k, v = map(r3, (z, xbc, dt, q, k, v))
    cst8 = jnp.pad(conv_state, ((0, 0), (SUBLANES - (D_CONV - 1), 0), (0, 0)))
    y_ssm, new_ssm, cout8 = _ssd(xbc, dt, z, cst8, ssm_state.reshape(b, SSM_WIDTH, D_STATE),
                                 mw["cw8"], mw["cb"], mw["dtb"], mw["alog"], mw["dsk"], mw["nw"],
                                 mw["expand"], ch)
    new_conv = cout8[:, SUBLANES - (D_CONV - 1):, :]
    if k_past is None:
        o_sb = _sb_prompt(q, k, v, _strict_upper(SB_BLOCK, BF16).T)
    else:
        past = k_past.shape[1]
        kb = min(512, past)
        qs = (q * SB_SCALE).reshape(b, l, SB_HEADS, HEAD_DIM)
        eye = jnp.eye(SB_HEADS, dtype=F32)
        qbd = jnp.einsum("bqhd,hg->bhdgq", qs, eye).reshape(b, SB_WIDTH, SB_HEADS * l).astype(BF16)
        o_sb = _sb_cached(qbd, k, v, k_past.reshape(b, past, SB_WIDTH), v_past.reshape(b, past, SB_WIDTH),
                          _strict_upper(l, BF16), _strict_upper(kb, BF16), kb)
    return y_ssm, o_sb, k, v, new_ssm, new_conv


NEG_INF = float("-inf")
_CAND_PIECES = (
    ((0, 16), (0, 1), None),
    ((0, 8), (1, 2), None),
    ((0, 1), (0, 16), (2, 15)),
    ((1, 2), (0, 8), (2, 7)),
    ((2, 3), (0, 8), (2, 4)),
    ((3, 4), (0, 8), (2, 3)),
    ((4, 5), (0, 8), (2, 2)),
)


def _topk_rows(s, key, k):
    t = s.shape[1]
    rid = lax.broadcasted_iota(jnp.int32, (k, t), 0)
    vals = jnp.zeros((k, t), F32)
    keys = jnp.zeros((k, t), jnp.int32)
    big = jnp.int32(2 ** 30)
    for r in range(k):
        m = jnp.max(s, axis=0, keepdims=True)
        sel_key = jnp.min(jnp.where(s == m, key, big), axis=0, keepdims=True)
        vals = jnp.where(rid == r, m, vals)
        keys = jnp.where(rid == r, sel_key, keys)
        s = jnp.where(key == sel_key, NEG_INF, s)
    return vals, keys


def _route_body(x_ref, ys_ref, os_ref, wo_ref, n2_ref, wqt_ref, k1_ref, k2_ref,
                h_ref, xn_ref, eid_ref, gate_ref):
    tt = x_ref.shape[0]
    h = (x_ref[...] + _dot(ys_ref[...].astype(BF16), wo_ref[0:SSM_WIDTH, :])
         + _dot(os_ref[...].astype(BF16), wo_ref[SSM_WIDTH:, :]))
    h_ref[...] = h
    xn = h * lax.rsqrt(jnp.mean(h * h, axis=-1, keepdims=True) + EPS) * n2_ref[...]
    xn_ref[...] = xn
    qt = _dot_nt(wqt_ref[...], xn.astype(BF16))

    kid = lax.broadcasted_iota(jnp.int32, (N_KEYS, tt), 0)
    eids, gates = [], []
    for hd in range(PEER_HEADS):
        q1 = qt[hd * D_KEY:hd * D_KEY + HALF_KEY, :].astype(BF16)
        q2 = qt[hd * D_KEY + HALF_KEY:(hd + 1) * D_KEY, :].astype(BF16)
        v1, i1 = _topk_rows(_dot(k1_ref[hd], q1), kid, PEER_TOPK)
        v2, i2 = _topk_rows(_dot(k2_ref[hd], q2), kid, PEER_TOPK)
        cand, cexp, cflat = [], [], []
        for (a0, a1), (b0, b1), valid in _CAND_PIECES:
            n = max(a1 - a0, b1 - b0)
            c = v1[a0:a1, :] + v2[b0:b1, :]
            e = i1[a0:a1, :] * N_KEYS + i2[b0:b1, :]
            ai = a0 + (lax.broadcasted_iota(jnp.int32, (n, tt), 0) if a1 - a0 > 1 else 0)
            bi = b0 + (lax.broadcasted_iota(jnp.int32, (n, tt), 0) if b1 - b0 > 1 else 0)
            f = ai * PEER_TOPK + bi + jnp.zeros((n, tt), jnp.int32)
            if valid is not None:
                c = jnp.where((bi >= valid[0]) & (bi <= valid[1]), c, NEG_INF)
            cand.append(c)
            cexp.append(e)
            cflat.append(f)
        cand = jnp.concatenate(cand, axis=0)
        cexp = jnp.concatenate(cexp, axis=0)
        cflat = jnp.concatenate(cflat, axis=0)
        best, bflat = _topk_rows(cand, cflat, PEER_TOPK)
        rid = lax.broadcasted_iota(jnp.int32, (PEER_TOPK, tt), 0)
        bexp = jnp.zeros((PEER_TOPK, tt), jnp.int32)
        for r in range(PEER_TOPK):
            e_r = jnp.max(jnp.where(cflat == bflat[r:r + 1, :], cexp, -1), axis=0, keepdims=True)
            bexp = jnp.where(rid == r, e_r, bexp)
        p = jnp.exp(best - best[0:1, :])
        gates.append(p / jnp.sum(p, axis=0, keepdims=True))
        eids.append(bexp)
    eid_ref[...] = jnp.concatenate(eids, axis=0).T * ROW_CHUNKS
    gate_ref[...] = jnp.concatenate(gates, axis=0).T


def _route(x2d, ys2d, os2d, w_out, n2, wqt, k1, k2, tt):
    n = x2d.shape[0]
    items = PEER_HEADS * PEER_TOPK
    row = lambda i: (i, 0)
    const2 = lambda i: (0, 0)
    const3 = lambda i: (0, 0, 0)
    return pl.pallas_call(
        _route_body,
        grid=(n // tt,),
        in_specs=[
            pl.BlockSpec((tt, D_MODEL), row),
            pl.BlockSpec((tt, SSM_WIDTH), row),
            pl.BlockSpec((tt, SB_WIDTH), row),
            pl.BlockSpec((SSM_WIDTH + SB_WIDTH, D_MODEL), const2),
            pl.BlockSpec((1, D_MODEL), const2),
            pl.BlockSpec((PEER_HEADS * D_KEY, D_MODEL), const2),
            pl.BlockSpec((PEER_HEADS, N_KEYS, HALF_KEY), const3),
            pl.BlockSpec((PEER_HEADS, N_KEYS, HALF_KEY), const3),
        ],
        out_specs=[
            pl.BlockSpec((tt, D_MODEL), row),
            pl.BlockSpec((tt, D_MODEL), row),
            pl.BlockSpec((tt, items), row),
            pl.BlockSpec((tt, items), row),
        ],
        out_shape=[
            jax.ShapeDtypeStruct((n, D_MODEL), F32),
            jax.ShapeDtypeStruct((n, D_MODEL), F32),
            jax.ShapeDtypeStruct((n, items), jnp.int32),
            jax.ShapeDtypeStruct((n, items), F32),
        ],
        compiler_params=pltpu.CompilerParams(
            dimension_semantics=("parallel",), vmem_limit_bytes=VMEM_LIMIT),
        name="peer_route",
    )(x2d, ys2d, os2d, w_out, n2, wqt, k1, k2)


HALF_MODEL = D_MODEL // 2
TOK_ROWS = SUBLANES
TOK_STRIDE = PEER_ITEMS + SUBLANES
GATHER_BUFS = 4


def _pack_table(tab):
    t = tab.astype(BF16).reshape(tab.shape[0], 2, ROW_CHUNKS, LANES)
    bits = lax.bitcast_convert_type(t, jnp.uint16).astype(jnp.uint32)
    words = bits[:, 0] | (bits[:, 1] << 16)
    return lax.bitcast_convert_type(words, jnp.int32).reshape(tab.shape[0] * ROW_CHUNKS, LANES)


def _pair_matrices():
    item = jnp.arange(PEER_ITEMS)[:, None]
    lane = jnp.arange(2 * PEER_ITEMS)[None, :]
    even = (lane == 2 * item).astype(BF16)
    odd = (lane == 2 * item + 1).astype(BF16)
    return even, odd, (even + odd).T


def _gather_token(erow_ref, t, tab_ref, buf):
    for i in range(PEER_ITEMS):
        src = pl.multiple_of(erow_ref[t, i], ROW_CHUNKS)
        buf[pl.ds(i, ROW_CHUNKS, stride=TOK_STRIDE), :] = tab_ref[pl.ds(src, ROW_CHUNKS), :]


def _token_pipeline(erow_ref, tab_ref, bufs, tb, compute):
    lead = GATHER_BUFS // 2
    for t in range(lead):
        _gather_token(erow_ref, t, tab_ref, bufs[t])

    def step(k, carry):
        t0 = GATHER_BUFS * k
        for j in range(GATHER_BUFS):
            nxt = jnp.minimum(t0 + j + lead, tb - 1)
            compute(t0 + j, bufs[j])
            _gather_token(erow_ref, nxt, tab_ref, bufs[(j + lead) % GATHER_BUFS])
        return carry
    lax.fori_loop(0, tb // GATHER_BUFS, step, 0)


def _chunk_bf16(buf, c):
    return pltpu.bitcast(buf[c * TOK_STRIDE:c * TOK_STRIDE + PEER_ITEMS, :], BF16)


def _spread_token_rows(dst, parts, tb):
    dst[...] = jnp.zeros_like(dst)
    for k, part in enumerate(parts):
        for j in range(dst.shape[0]):
            dst[j, pl.ds(k, tb, stride=TOK_ROWS), :] = part[:, j * LANES:(j + 1) * LANES]


def _token_rows(src, j, t):
    return src[j, pl.ds(pl.multiple_of(t * TOK_ROWS, TOK_ROWS), TOK_ROWS), :]


def _peer_u_body(erow_ref, x_ref, tab_ref, psum_ref, act_ref, b0, b1, b2, b3, xbuf, pbuf, *, tb):
    x = x_ref[...]
    xh = x.astype(BF16).astype(F32)
    xl = x - xh
    _spread_token_rows(xbuf, (xh[:, :HALF_MODEL], xl[:, :HALF_MODEL], xh[:, HALF_MODEL:], xl[:, HALF_MODEL:]), tb)
    even_lane = lax.broadcasted_iota(jnp.int32, (1, 2 * PEER_ITEMS), 1) % 2 == 0

    def compute(t, buf):
        acc = jnp.zeros((TOK_ROWS, 2 * PEER_ITEMS), F32)
        for c in range(ROW_CHUNKS):
            acc = acc + _dot_nt(_token_rows(xbuf, c, t).astype(BF16), _chunk_bf16(buf, c))
        pbuf[pl.ds(t, 1), :] = jnp.where(even_lane, acc[0:1] + acc[1:2], acc[2:3] + acc[3:4])

    _token_pipeline(erow_ref, tab_ref, (b0, b1, b2, b3), tb, compute)
    ph, plo = _split_bf16(pbuf[...])
    act_ref[...] = _dot(ph, psum_ref[...]) + _dot(plo, psum_ref[...])


def _gelu_exact(x):
    return 0.5 * x * (1.0 + lax.erf(x * (2.0 ** -0.5)))


def _peer_v_body(erow_ref, act_ref, gate_ref, h_ref, fnw_ref, even_ref, odd_ref, tab_ref, y_ref,
                 b0, b1, b2, b3, cbuf, pebuf, *, tb):
    coef = gate_ref[...] * _gelu_exact(act_ref[...])
    ch, cl = _split_bf16(coef)
    _spread_token_rows(cbuf, (_dot(ch, even_ref[...]), _dot(cl, even_ref[...]),
                              _dot(ch, odd_ref[...]), _dot(cl, odd_ref[...])), tb)

    def compute(t, buf):
        cb = jnp.concatenate([_token_rows(cbuf, 0, t), _token_rows(cbuf, 1, t)], axis=1).astype(BF16)
        lo, hi = [], []
        for c in range(ROW_CHUNKS):
            r = _dot(cb, _chunk_bf16(buf, c))
            lo.append(r[0:1] + r[1:2])
            hi.append(r[2:3] + r[3:4])
        pebuf[pl.ds(t, 1), :] = jnp.concatenate(lo + hi, axis=1)

    _token_pipeline(erow_ref, tab_ref, (b0, b1, b2, b3), tb, compute)
    y = h_ref[...] + pebuf[...]
    y_ref[...] = y * lax.rsqrt(jnp.mean(y * y, axis=-1, keepdims=True) + EPS) * fnw_ref[...]


def _peer_specs(tb):
    row = lambda i: (i, 0)
    const = lambda i: (0, 0)
    erow_spec = pl.BlockSpec((tb, PEER_ITEMS), row, memory_space=pltpu.SMEM)
    tab_spec = pl.BlockSpec((N_KEYS * N_KEYS * ROW_CHUNKS, LANES), const, pipeline_mode=pl.Buffered(1))
    tok_bufs = [pltpu.VMEM((ROW_CHUNKS * TOK_STRIDE, LANES), jnp.int32)] * GATHER_BUFS
    params = pltpu.CompilerParams(dimension_semantics=("arbitrary",), vmem_limit_bytes=VMEM_LIMIT)
    return row, const, erow_spec, tab_spec, tok_bufs, params


def _peer_u(erow, xn, u_packed, pair_sum, tb):
    n = xn.shape[0]
    row, const, erow_spec, tab_spec, tok_bufs, params = _peer_specs(tb)
    return pl.pallas_call(
        functools.partial(_peer_u_body, tb=tb),
        grid=(n // tb,),
        in_specs=[erow_spec, pl.BlockSpec((tb, D_MODEL), row), tab_spec,
                  pl.BlockSpec((2 * PEER_ITEMS, PEER_ITEMS), const)],
        out_specs=pl.BlockSpec((tb, PEER_ITEMS), row),
        out_shape=jax.ShapeDtypeStruct((n, PEER_ITEMS), F32),
        scratch_shapes=tok_bufs + [pltpu.VMEM((ROW_CHUNKS, tb * TOK_ROWS, LANES), F32),
                                   pltpu.VMEM((tb, 2 * PEER_ITEMS), F32)],
        compiler_params=params,
        name="peer_u",
    )(erow, xn, u_packed, pair_sum)


def _peer_v(erow, act, gate, h, fnw, v_packed, even, odd, tb):
    n = h.shape[0]
    row, const, erow_spec, tab_spec, tok_bufs, params = _peer_specs(tb)
    item_spec = pl.BlockSpec((tb, PEER_ITEMS), row)
    mat_spec = pl.BlockSpec((PEER_ITEMS, 2 * PEER_ITEMS), const)
    return pl.pallas_call(
        functools.partial(_peer_v_body, tb=tb),
        grid=(n // tb,),
        in_specs=[erow_spec, item_spec, item_spec, pl.BlockSpec((tb, D_MODEL), row),
                  pl.BlockSpec((1, D_MODEL), const), mat_spec, mat_spec, tab_spec],
        out_specs=pl.BlockSpec((tb, D_MODEL), row),
        out_shape=jax.ShapeDtypeStruct((n, D_MODEL), F32),
        scratch_shapes=tok_bufs + [pltpu.VMEM((2, tb * TOK_ROWS, LANES), F32),
                                   pltpu.VMEM((tb, D_MODEL), F32)],
        compiler_params=params,
        name="peer_v",
    )(erow, act, gate, h, fnw, even, odd, v_packed)


def _channel_mixer(x2d, ys2d, os2d, cw, tt, tb):
    h, xn, erow, gate = _route(x2d, ys2d, os2d, cw["w_out"], cw["n2"], cw["wqt"], cw["k1"], cw["k2"], tt)
    even, odd, pair_sum = _pair_matrices()
    act = _peer_u(erow, xn, cw["u"], pair_sum, tb)
    return _peer_v(erow, act, gate, h, cw["fnw"], cw["v"], even, odd, tb)


def kernel(x_prompt, x_sample, cache_sb_k, cache_sb_v, state_ssm, state_conv, norm1_w, w_in, conv_w,
           conv_b, dt_bias, a_log, d_skip, ssm_norm_w, w_out, norm2_w, peer_wq, peer_k1, peer_k2,
           peer_u, peer_v, final_norm_w):
    assert norm1_w.shape[0] == 1, "single layer"
    bp, lp, _ = x_prompt.shape
    bs, ls, _ = x_sample.shape
    mw = _prep_mixer_weights(norm1_w[0], w_in[0], conv_w[0], conv_b[0], dt_bias[0], a_log[0],
                             d_skip[0], ssm_norm_w[0])
    cw = dict(w_out=w_out[0].astype(BF16), n2=norm2_w[0].reshape(1, D_MODEL),
              wqt=peer_wq[0].T.astype(BF16), k1=peer_k1[0].astype(BF16), k2=peer_k2[0].astype(BF16),
              u=_pack_table(peer_u[0]), v=_pack_table(peer_v[0]), fnw=final_norm_w.reshape(1, D_MODEL))

    conv0 = jnp.zeros((bp, D_CONV - 1, CONV_DIM), F32)
    ssm0 = jnp.zeros((bp, SSM_HEADS, HEAD_DIM, D_STATE), F32)
    ysp, osp, kp, vp, sp, cp = _mixer(x_prompt, conv0, ssm0, None, None, mw, min(256, lp), min(512, bp * lp))
    yss, oss, ks, vs, ss, cs = _mixer(x_sample, state_conv[0], state_ssm[0], cache_sb_k[0], cache_sb_v[0],
                                      mw, ls, min(512, bs * ls))
    npr, nsa = bp * lp, bs * ls
    x_all = jnp.concatenate([x_prompt.reshape(npr, D_MODEL), x_sample.reshape(nsa, D_MODEL)], axis=0)
    ys_all = jnp.concatenate([ysp.reshape(npr, SSM_WIDTH), yss.reshape(nsa, SSM_WIDTH)], axis=0)
    os_all = jnp.concatenate([osp.reshape(npr, SB_WIDTH), oss.reshape(nsa, SB_WIDTH)], axis=0)
    y_all = _channel_mixer(x_all, ys_all, os_all, cw, 256, 64)
    y_prompt = y_all[:npr].reshape(bp, lp, D_MODEL)
    y_sample = y_all[npr:].reshape(bs, ls, D_MODEL)
    hd = lambda t, b, l: t.reshape(1, b, l, SB_HEADS, HEAD_DIM)
    return (y_prompt, y_sample, hd(kp, bp, lp), hd(vp, bp, lp),
            sp.reshape(1, bp, SSM_HEADS, HEAD_DIM, D_STATE), cp[None],
            hd(ks, bs, ls), hd(vs, bs, ls),
            ss.reshape(1, bs, SSM_HEADS, HEAD_DIM, D_STATE), cs[None])
```
